```python
import math
import jax, jax.numpy as jnp
from jax import lax
import numpy as np

D_MODEL = 2048
BATCH = 2
SEQ = 8192
DEPTH = 2

CONV_CH = 1024
CONV_K = 31
N_ATT_HEADS = 4
ATT_HD = 128
ATT_QK = N_ATT_HEADS * 2 * ATT_HD
ATT_V = N_ATT_HEADS * 2 * ATT_HD
MIX_IN = 2 * CONV_CH + 2 * ATT_QK + ATT_V + 2 * D_MODEL
MEM_LEN = 256
N_MEM_HEADS = 4
MEM_HD = 128
MEM_W = N_MEM_HEADS * MEM_HD
D_FF = 5632
QBLK = 128
ROPE_THETA = 10000.0
EPS = 1e-6
LN_EPS = 1e-5

kernel_name = "hybrid_conformer_diffattn_block"


def rmsnorm(x, g):
    xf = x.astype(jnp.float32)
    y = xf * lax.rsqrt(jnp.mean(xf * xf, axis=-1, keepdims=True) + EPS)
    return (y * g.astype(jnp.float32)).astype(x.dtype)


def swiglu_ffn(x, g, w_in, w_out):
    h = rmsnorm(x, g) @ w_in
    a, b = jnp.split(h, 2, axis=-1)
    return (jax.nn.silu(a) * b) @ w_out


def rope(t, cos, sin):
    tf = t.astype(jnp.float32)
    t1, t2 = jnp.split(tf, 2, axis=-1)
    out = jnp.concatenate([t1 * cos - t2 * sin, t2 * cos + t1 * sin], axis=-1)
    return out.astype(t.dtype)


def conv_branch(u, conv_w, conv_b, ln_g, ln_b, w_out):
    a, b = jnp.split(u, 2, axis=-1)
    g = a * jax.nn.sigmoid(b)
    y = lax.conv_general_dilated(
        g, conv_w[:, None, :], window_strides=(1,), padding=((CONV_K - 1, 0),),
        dimension_numbers=("NWC", "WIO", "NWC"), feature_group_count=CONV_CH)
    y = y + conv_b
    yf = y.astype(jnp.float32)
    mu = jnp.mean(yf, axis=-1, keepdims=True)
    var = jnp.mean(jnp.square(yf - mu), axis=-1, keepdims=True)
    yf = (yf - mu) * lax.rsqrt(var + LN_EPS) * ln_g.astype(jnp.float32) + ln_b.astype(jnp.float32)
    y = jax.nn.silu(yf).astype(u.dtype)
    return y @ w_out


def diff_attention(q, k, v, lam):
    B, S = q.shape[0], q.shape[1]
    nblk = S // QBLK
    qb = q.reshape(B, nblk, QBLK, N_ATT_HEADS, 2, ATT_HD).transpose(1, 0, 2, 3, 4, 5)
    kf = k.astype(jnp.float32)
    key_pos = jnp.arange(S, dtype=jnp.int32)

    def block(args):
        qi, bi = args
        s = jnp.einsum("bqhmd,bkhmd->bhmqk", qi.astype(jnp.float32), kf)
        q_pos = bi * QBLK + jnp.arange(QBLK, dtype=jnp.int32)
        mask = key_pos[None, :] <= q_pos[:, None]
        s = jnp.where(mask, s, -jnp.inf)
        p = jax.nn.softmax(s, axis=-1)
        a = p[:, :, 0] - lam * p[:, :, 1]
        return jnp.einsum("bhqk,bkhe->bqhe", a.astype(v.dtype), v)

    out = lax.map(block, (qb, jnp.arange(nblk, dtype=jnp.int32)))
    return out.transpose(1, 0, 2, 3, 4).reshape(B, S, N_ATT_HEADS, 2 * ATT_HD)


def cross_attention(x, mem, g_x, g_m, w_q, w_kv, w_o):
    B, S = x.shape[0], x.shape[1]
    q = (rmsnorm(x, g_x) @ w_q).reshape(B, S, N_MEM_HEADS, MEM_HD)
    kv = rmsnorm(mem, g_m) @ w_kv
    k, v = jnp.split(kv, 2, axis=-1)
    k = k.reshape(B, MEM_LEN, N_MEM_HEADS, MEM_HD)
    v = v.reshape(B, MEM_LEN, N_MEM_HEADS, MEM_HD)
    s = jnp.einsum("bqhd,bkhd->bhqk", q.astype(jnp.float32), k.astype(jnp.float32)) * (MEM_HD ** -0.5)
    p = jax.nn.softmax(s, axis=-1)
    o = jnp.einsum("bhqk,bkhd->bqhd", p.astype(v.dtype), v).reshape(B, S, MEM_W)
    return o @ w_o


def setup_inputs(seed: int = 0) -> dict:
    key = jax.random.key(seed)
    ks = iter(jax.random.split(key, 40))
    f32 = jnp.float32

    def w(shape, fan_in):
        return jax.random.normal(next(ks), shape, f32) * (fan_in ** -0.5)

    def gain(shape):
        return 1.0 + 0.02 * jax.random.normal(next(ks), shape, f32)

    def small(shape):
        return 0.01 * jax.random.normal(next(ks), shape, f32)

    L = DEPTH
    x = jax.random.normal(next(ks), (BATCH, SEQ, D_MODEL), f32)
    mem = jax.random.normal(next(ks), (BATCH, MEM_LEN, D_MODEL), f32)
    positions = jnp.tile(jnp.arange(SEQ, dtype=jnp.int32)[None, :], (BATCH, 1))
    return {
        "x": x,
        "mem": mem,
        "positions": positions,
        "ffn1_norm": gain((L, D_MODEL)),
        "ffn1_w_in": w((L, D_MODEL, 2 * D_FF), D_MODEL),
        "ffn1_w_out": w((L, D_FF, D_MODEL), D_FF),
        "mix_norm": gain((L, D_MODEL)),
        "mix_w_in": w((L, D_MODEL, MIX_IN), D_MODEL),
        "conv_w": w((L, CONV_K, CONV_CH), CONV_K),
        "conv_b": small((L, CONV_CH)),
        "conv_ln_g": gain((L, CONV_CH)),
        "conv_ln_b": small((L, CONV_CH)),
        "conv_w_out": w((L, CONV_CH, D_MODEL), CONV_CH),
        "diff_lambda": 0.1 * jax.random.normal(next(ks), (L, 4, ATT_HD), f32),
        "diff_subln_g": gain((L, 2 * ATT_HD)),
        "diff_w_out": w((L, ATT_V, D_MODEL), ATT_V),
        "mix_w_out": w((L, D_MODEL, D_MODEL), D_MODEL),
        "cross_norm": gain((L, D_MODEL)),
        "mem_norm": gain((L, D_MODEL)),
        "cross_w_q": w((L, D_MODEL, MEM_W), D_MODEL),
        "cross_w_kv": w((L, D_MODEL, 2 * MEM_W), D_MODEL),
        "cross_w_o": w((L, MEM_W, D_MODEL), MEM_W),
        "ffn2_norm": gain((L, D_MODEL)),
        "ffn2_w_in": w((L, D_MODEL, 2 * D_FF), D_MODEL),
        "ffn2_w_out": w((L, D_FF, D_MODEL), D_FF),
        "final_norm": gain((D_MODEL,)),
    }


def reference(x, mem, positions, ffn1_norm, ffn1_w_in, ffn1_w_out, mix_norm, mix_w_in,
              conv_w, conv_b, conv_ln_g, conv_ln_b, conv_w_out, diff_lambda, diff_subln_g,
              diff_w_out, mix_w_out, cross_norm, mem_norm, cross_w_q, cross_w_kv, cross_w_o,
              ffn2_norm, ffn2_w_in, ffn2_w_out, final_norm):
    B, S = x.shape[0], x.shape[1]
    inv_freq = ROPE_THETA ** (-jnp.arange(0, ATT_HD, 2, dtype=jnp.float32) / ATT_HD)
    ang = positions.astype(jnp.float32)[..., None] * inv_freq
    cos = jnp.cos(ang)[:, :, None, None, :]
    sin = jnp.sin(ang)[:, :, None, None, :]
    splits = [2 * CONV_CH,
              2 * CONV_CH + ATT_QK,
              2 * CONV_CH + 2 * ATT_QK,
              2 * CONV_CH + 2 * ATT_QK + ATT_V,
              2 * CONV_CH + 2 * ATT_QK + ATT_V + D_MODEL]

    for i in range(DEPTH):
        x = x + 0.5 * swiglu_ffn(x, ffn1_norm[i], ffn1_w_in[i], ffn1_w_out[i])

        h = rmsnorm(x, mix_norm[i]) @ mix_w_in[i]
        u_conv, q, k, v, g_conv, g_attn = jnp.split(h, splits, axis=-1)

        y_conv = conv_branch(u_conv, conv_w[i], conv_b[i], conv_ln_g[i], conv_ln_b[i], conv_w_out[i])

        q = rope(q.reshape(B, S, N_ATT_HEADS, 2, ATT_HD), cos, sin) * (ATT_HD ** -0.5)
        k = rope(k.reshape(B, S, N_ATT_HEADS, 2, ATT_HD), cos, sin)
        v = v.reshape(B, S, N_ATT_HEADS, 2 * ATT_HD)
        lam_init = 0.8 - 0.6 * math.exp(-0.3 * i)
        lp = diff_lambda[i].astype(jnp.float32)
        lam = jnp.exp(jnp.sum(lp[0] * lp[1])) - jnp.exp(jnp.sum(lp[2] * lp[3])) + lam_init
        o = diff_attention(q, k, v, lam)
        o = rmsnorm(o, diff_subln_g[i]) * (1.0 - lam_init)
        y_attn = o.reshape(B, S, ATT_V) @ diff_w_out[i]

        merged = jax.nn.sigmoid(g_conv) * y_conv + jax.nn.sigmoid(g_attn) * y_attn
        x = x + merged @ mix_w_out[i]

        x = x + cross_attention(x, mem, cross_norm[i], mem_norm[i], cross_w_q[i], cross_w_kv[i], cross_w_o[i])

        x = x + 0.5 * swiglu_ffn(x, ffn2_norm[i], ffn2_w_in[i], ffn2_w_out[i])

    return rmsnorm(x, final_norm)
```

```python
import functools
import math

import jax
import jax.numpy as jnp
from jax import lax
from jax.experimental import pallas as pl
from jax.experimental.pallas import tpu as pltpu

F32 = jnp.float32
BF16 = jnp.bfloat16

CONV_CH = 1024
CONV_K = 31
N_HEADS = 4
HEAD_DIM = 128
HEAD_W = 2 * HEAD_DIM
QK_W = N_HEADS * HEAD_W
MEM_HEADS = 4
MEM_HD = 128
MEM_W = MEM_HEADS * MEM_HD
ROPE_THETA = 10000.0
EPS = 1e-6
LN_EPS = 1e-5

LANES_V7X = 128
SUBLANES_V7X = 8
VMEM_BYTES_V7X = 64 * 1024 * 1024
CONV_HALO = 32


def _vmem_limit(block_bytes):
    want = 2 * block_bytes + 12 * 1024 * 1024
    return int(min(want, VMEM_BYTES_V7X - 6 * 1024 * 1024))


def _nbytes(shape, dtype):
    return math.prod(shape) * jnp.dtype(dtype).itemsize


def _rms(xf, g):
    ms = jnp.mean(xf * xf, axis=-1, keepdims=True)
    return xf * lax.rsqrt(ms + EPS) * g


def _sigmoid(x):
    return 1.0 / (1.0 + jnp.exp(-x))


def _rmsnorm_kernel(x_ref, g_ref, o_ref):
    o_ref[...] = _rms(x_ref[...], g_ref[...]).astype(o_ref.dtype)


def _rmsnorm(x, g, out_dtype, tm=512):
    n, d = x.shape
    return pl.pallas_call(
        _rmsnorm_kernel,
        out_shape=jax.ShapeDtypeStruct((n, d), out_dtype),
        grid=(n // tm,),
        in_specs=[pl.BlockSpec((tm, d), lambda i: (i, 0)),
                  pl.BlockSpec((1, d), lambda i: (0, 0))],
        out_specs=pl.BlockSpec((tm, d), lambda i: (i, 0)),
        compiler_params=pltpu.CompilerParams(
            dimension_semantics=("parallel",),
            vmem_limit_bytes=_vmem_limit(_nbytes((tm, d), F32) + _nbytes((tm, d), out_dtype))),
        name="rmsnorm",
    )(x, g.reshape(1, d))


def _rope_table_kernel(pos_ref, inv_ref, cs_ref, sn_ref):
    ang = pos_ref[...].astype(F32) * inv_ref[...]
    lane = lax.broadcasted_iota(jnp.int32, ang.shape, 1)
    s = jnp.sin(ang)
    cs_ref[...] = jnp.cos(ang)
    sn_ref[...] = jnp.where(lane < HEAD_DIM // 2, -s, s)


def _rope_tables(pos, inv_freq2, tm=1024):
    n = pos.shape[0]
    spec = pl.BlockSpec((tm, HEAD_DIM), lambda i: (i, 0))
    return pl.pallas_call(
        _rope_table_kernel,
        out_shape=(jax.ShapeDtypeStruct((n, HEAD_DIM), F32),) * 2,
        grid=(n // tm,),
        in_specs=[pl.BlockSpec((tm, 1), lambda i: (i, 0)),
                  pl.BlockSpec((1, HEAD_DIM), lambda i: (0, 0))],
        out_specs=(spec, spec),
        compiler_params=pltpu.CompilerParams(dimension_semantics=("parallel",)),
        name="rope_tables",
    )(pos, inv_freq2)


def _ffn_kernel(x_ref, xn_ref, wa_ref, wb_ref, wo_ref, g_ref, xo_ref, xno_ref, *, nj):
    j = pl.program_id(1)
    xn = xn_ref[...]
    a = jnp.dot(xn, wa_ref[...], preferred_element_type=F32)
    b = jnp.dot(xn, wb_ref[...], preferred_element_type=F32)
    act = ((0.5 * a) * _sigmoid(a) * b).astype(BF16)
    contrib = jnp.dot(act, wo_ref[...], preferred_element_type=F32)

    @pl.when(j == 0)
    def _():
        xo_ref[...] = x_ref[...] + contrib

    @pl.when(j > 0)
    def _():
        xo_ref[...] += contrib

    @pl.when(j == nj - 1)
    def _():
        xno_ref[...] = _rms(xo_ref[...], g_ref[...]).astype(xno_ref.dtype)


def _ffn(x, xn, w_in, w_out, g_next, xn_dtype, tm=512, tf=512):
    n, d = x.shape
    f = w_out.shape[0]
    nj = f // tf
    blocks = (_nbytes((tm, d), F32) * 2 + _nbytes((tm, d), BF16) + _nbytes((tm, d), xn_dtype)
              + 2 * _nbytes((d, tf), BF16) + _nbytes((tf, d), BF16))
    return pl.pallas_call(
        functools.partial(_ffn_kernel, nj=nj),
        out_shape=(jax.ShapeDtypeStruct((n, d), F32), jax.ShapeDtypeStruct((n, d), xn_dtype)),
        grid=(n // tm, nj),
        in_specs=[pl.BlockSpec((tm, d), lambda i, j: (i, 0)),
                  pl.BlockSpec((tm, d), lambda i, j: (i, 0)),
                  pl.BlockSpec((d, tf), lambda i, j: (0, j)),
                  pl.BlockSpec((d, tf), lambda i, j: (0, j + nj)),
                  pl.BlockSpec((tf, d), lambda i, j: (j, 0)),
                  pl.BlockSpec((1, d), lambda i, j: (0, 0))],
        out_specs=(pl.BlockSpec((tm, d), lambda i, j: (i, 0)),
                   pl.BlockSpec((tm, d), lambda i, j: (i, 0))),
        compiler_params=pltpu.CompilerParams(
            dimension_semantics=("parallel", "arbitrary"),
            vmem_limit_bytes=_vmem_limit(blocks)),
        name="ffn",
    )(x, xn, w_in, w_in, w_out, g_next.reshape(1, d))


def _glu_kernel(xn_ref, wa_ref, wb_ref, o_ref):
    xn = xn_ref[...]
    a = jnp.dot(xn, wa_ref[...], preferred_element_type=F32)
    b = jnp.dot(xn, wb_ref[...], preferred_element_type=F32)
    o_ref[...] = (a * _sigmoid(b)).astype(o_ref.dtype)


def _proj_glu(xn, w, tm=1024, tn=512):
    n, d = xn.shape
    nb = CONV_CH // tn
    blocks = _nbytes((tm, d), BF16) + 2 * _nbytes((d, tn), BF16) + _nbytes((tm, tn), F32)
    return pl.pallas_call(
        _glu_kernel,
        out_shape=jax.ShapeDtypeStruct((n, CONV_CH), F32),
        grid=(n // tm, nb),
        in_specs=[pl.BlockSpec((tm, d), lambda i, j: (i, 0)),
                  pl.BlockSpec((d, tn), lambda i, j: (0, j)),
                  pl.BlockSpec((d, tn), lambda i, j: (0, j + nb))],
        out_specs=pl.BlockSpec((tm, tn), lambda i, j: (i, j)),
        compiler_params=pltpu.CompilerParams(
            dimension_semantics=("parallel", "arbitrary"),
            vmem_limit_bytes=_vmem_limit(blocks)),
        name="proj_glu",
    )(xn, w, w)


def _rope_proj_kernel(xn_ref, w_ref, cs_ref, sn_ref, o_ref, *, n_q_blocks):
    j = pl.program_id(1)
    t = jnp.dot(xn_ref[...], w_ref[...], preferred_element_type=F32)
    cs = cs_ref[...]
    sn = sn_ref[...]
    scale = jnp.where(j < n_q_blocks, HEAD_DIM ** -0.5, 1.0).astype(F32)
    for c in range(t.shape[1] // HEAD_DIM):
        tc = t[:, c * HEAD_DIM:(c + 1) * HEAD_DIM]
        r = tc * cs + pltpu.roll(tc, HEAD_DIM // 2, 1) * sn
        o_ref[:, c * HEAD_DIM:(c + 1) * HEAD_DIM] = (r * scale).astype(o_ref.dtype)


def _proj_rope(xn, w, cs, sn, col0, tm=1024, tn=1024):
    n, d = xn.shape
    width = 2 * QK_W
    blocks = (_nbytes((tm, d), BF16) + _nbytes((d, tn), BF16) + _nbytes((tm, tn), BF16)
              + 2 * _nbytes((tm, HEAD_DIM), F32) + _nbytes((tm, tn), F32))
    return pl.pallas_call(
        functools.partial(_rope_proj_kernel, n_q_blocks=QK_W // tn),
        out_shape=jax.ShapeDtypeStruct((n, width), BF16),
        grid=(n // tm, width // tn),
        in_specs=[pl.BlockSpec((tm, d), lambda i, j: (i, 0)),
                  pl.BlockSpec((d, tn), lambda i, j: (0, col0 // tn + j)),
                  pl.BlockSpec((tm, HEAD_DIM), lambda i, j: (i, 0)),
                  pl.BlockSpec((tm, HEAD_DIM), lambda i, j: (i, 0))],
        out_specs=pl.BlockSpec((tm, tn), lambda i, j: (i, j)),
        compiler_params=pltpu.CompilerParams(
            dimension_semantics=("parallel", "arbitrary"),
            vmem_limit_bytes=_vmem_limit(blocks)),
        name="proj_rope",
    )(xn, w, cs, sn)


def _plain_proj_kernel(xn_ref, w_ref, o_ref, *, gate):
    t = jnp.dot(xn_ref[...], w_ref[...], preferred_element_type=F32)
    if gate:
        t = _sigmoid(t)
    o_ref[...] = t.astype(o_ref.dtype)


def _proj_plain(xn, w, col0, width, gate, tm=1024, tn=1024):
    n, d = xn.shape
    blocks = (_nbytes((tm, d), BF16) + _nbytes((d, tn), BF16) + _nbytes((tm, tn), BF16)
              + _nbytes((tm, tn), F32))
    return pl.pallas_call(
        functools.partial(_plain_proj_kernel, gate=gate),
        out_shape=jax.ShapeDtypeStruct((n, width), BF16),
        grid=(n // tm, width // tn),
        in_specs=[pl.BlockSpec((tm, d), lambda i, j: (i, 0)),
                  pl.BlockSpec((d, tn), lambda i, j: (0, col0 // tn + j))],
        out_specs=pl.BlockSpec((tm, tn), lambda i, j: (i, j)),
        compiler_params=pltpu.CompilerParams(
            dimension_semantics=("parallel", "arbitrary"),
            vmem_limit_bytes=_vmem_limit(blocks)),
        name="proj_gate" if gate else "proj_v",
    )(xn, w)


def _conv_kernel(halo_ref, g_ref, w_ref, b_ref, lng_ref, lnb_ref, o_ref, cat_ref, y_ref, *, tm, rc, cc):
    i = pl.program_id(1)
    cat_ref[0:CONV_HALO, :] = jnp.where(i == 0, 0.0, halo_ref[...])
    cat_ref[CONV_HALO:CONV_HALO + tm, :] = g_ref[...]
    first = CONV_HALO - (CONV_K - 1)
    for r0 in range(0, tm, rc):
        for c0 in range(0, CONV_CH, cc):
            acc = jnp.zeros((rc, cc), F32)
            for k in range(CONV_K):
                tap = cat_ref[first + k + r0:first + k + r0 + rc, c0:c0 + cc]
                acc = acc + w_ref[k:k + 1, c0:c0 + cc] * tap
            y_ref[r0:r0 + rc, c0:c0 + cc] = acc + b_ref[:, c0:c0 + cc]
    y = y_ref[...]
    mu = jnp.mean(y, axis=-1, keepdims=True)
    yc = y - mu
    var = jnp.mean(yc * yc, axis=-1, keepdims=True)
    yn = yc * lax.rsqrt(var + LN_EPS) * lng_ref[...] + lnb_ref[...]
    o_ref[...] = (yn * _sigmoid(yn)).astype(o_ref.dtype)


def _conv_branch(g, conv_w, conv_b, ln_g, ln_b, batch, tm=256, rc=64, cc=256):
    n, c = g.shape
    seq = n // batch
    nt = seq // tm
    hb = tm // CONV_HALO
    blocks = (_nbytes((tm, c), F32) + _nbytes((CONV_HALO, c), F32) + _nbytes((tm, c), BF16)
              + _nbytes((CONV_HALO + tm, c), F32) + _nbytes((tm, c), F32))
    row = lambda v: v.reshape(1, c)
    vec = pl.BlockSpec((1, c), lambda b, i: (0, 0))
    return pl.pallas_call(
        functools.partial(_conv_kernel, tm=tm, rc=rc, cc=cc),
        out_shape=jax.ShapeDtypeStruct((n, c), BF16),
        grid=(batch, nt),
        in_specs=[pl.BlockSpec((CONV_HALO, c), lambda b, i: (jnp.maximum((b * nt + i) * hb - 1, 0), 0)),
                  pl.BlockSpec((tm, c), lambda b, i: (b * nt + i, 0)),
                  pl.BlockSpec((CONV_K, c), lambda b, i: (0, 0)),
                  vec, vec, vec],
        out_specs=pl.BlockSpec((tm, c), lambda b, i: (b * nt + i, 0)),
        scratch_shapes=[pltpu.VMEM((CONV_HALO + tm, c), F32), pltpu.VMEM((tm, c), F32)],
        compiler_params=pltpu.CompilerParams(
            dimension_semantics=("parallel", "parallel"),
            vmem_limit_bytes=_vmem_limit(blocks)),
        name="conv_branch",
    )(g, g, conv_w, row(conv_b), row(ln_g), row(ln_b))


def _attn_kernel(lam_ref, gsub_ref, q_ref, k_ref, v_ref, o_ref, m_ref, l_ref, acc_ref, *, tq, lam_init):
    i = pl.program_id(2)
    m_ref[...] = jnp.full(m_ref.shape, -jnp.inf, F32)
    l_ref[...] = jnp.zeros(l_ref.shape, F32)
    acc_ref[...] = jnp.zeros(acc_ref.shape, F32)

    def step(j, masked):
        start = pl.multiple_of(j * tq, tq)
        vs = v_ref[pl.ds(start, tq), :]
        for mp in range(2):
            lo, hi = mp * HEAD_DIM, (mp + 1) * HEAD_DIM
            s = lax.dot_general(q_ref[:, lo:hi], k_ref[pl.ds(start, tq), lo:hi],
                                (((1,), (1,)), ((), ())), preferred_element_type=F32)
            if masked:
                row = lax.broadcasted_iota(jnp.int32, s.shape, 0)
                col = lax.broadcasted_iota(jnp.int32, s.shape, 1)
                s = jnp.where(col <= row, s, -jnp.inf)
            m_old = m_ref[mp]
            m_new = jnp.maximum(m_old, jnp.max(s, axis=-1, keepdims=True))
            p = jnp.exp(s - m_new)
            alpha = jnp.exp(m_old - m_new)
            l_ref[mp] = alpha * l_ref[mp] + jnp.sum(p, axis=-1, keepdims=True)
            acc_ref[mp] = alpha * acc_ref[mp] + jnp.dot(p.astype(BF16), vs, preferred_element_type=F32)
            m_ref[mp] = m_new

    def body(j, carry):
        step(j, masked=False)
        return carry

    lax.fori_loop(0, i, body, 0)
    step(i, masked=True)

    lp = lam_ref[...]
    lam = (jnp.exp(jnp.sum(lp[0:1] * lp[1:2], axis=-1, keepdims=True))
           - jnp.exp(jnp.sum(lp[2:3] * lp[3:4], axis=-1, keepdims=True)) + lam_init)
    o = acc_ref[0] / l_ref[0] - lam * (acc_ref[1] / l_ref[1])
    o_ref[...] = (_rms(o, gsub_ref[...]) * (1.0 - lam_init)).astype(o_ref.dtype)


def _diff_attention(qk, v, lam_p, g_sub, lam_init, batch, tq=512):
    n = v.shape[0]
    seq = n // batch
    nq = seq // tq
    blocks = (2 * _nbytes((tq, HEAD_W), BF16) + 2 * _nbytes((seq, HEAD_W), BF16)
              + 3 * _nbytes((2, tq, HEAD_W), F32) + 4 * _nbytes((tq, tq), F32))
    return pl.pallas_call(
        functools.partial(_attn_kernel, tq=tq, lam_init=lam_init),
        out_shape=jax.ShapeDtypeStruct((n, QK_W), BF16),
        grid=(batch, N_HEADS, nq),
        in_specs=[pl.BlockSpec((4, HEAD_DIM), lambda b, h, i: (0, 0)),
                  pl.BlockSpec((1, HEAD_W), lambda b, h, i: (0, 0)),
                  pl.BlockSpec((tq, HEAD_W), lambda b, h, i: (b * nq + i, h)),
                  pl.BlockSpec((seq, HEAD_W), lambda b, h, i: (b, N_HEADS + h)),
                  pl.BlockSpec((seq, HEAD_W), lambda b, h, i: (b, h))],
        out_specs=pl.BlockSpec((tq, HEAD_W), lambda b, h, i: (b * nq + i, h)),
        scratch_shapes=[pltpu.VMEM((2, tq, 1), F32), pltpu.VMEM((2, tq, 1), F32),
                        pltpu.VMEM((2, tq, HEAD_W), F32)],
        compiler_params=pltpu.CompilerParams(
            dimension_semantics=("parallel", "parallel", "arbitrary"),
            vmem_limit_bytes=_vmem_limit(blocks)),
        name="diff_attention",
    )(lam_p, g_sub.reshape(1, HEAD_W), qk, qk, v)


def _merge_kernel(x_ref, c_ref, o_ref, gc_ref, ga_ref, wc_ref, wd_ref, wm_ref, xo_ref):
    y_conv = jnp.dot(c_ref[...], wc_ref[...], preferred_element_type=F32)
    y_attn = jnp.dot(o_ref[...], wd_ref[...], preferred_element_type=F32)
    merged = gc_ref[...].astype(F32) * y_conv + ga_ref[...].astype(F32) * y_attn
    xo_ref[...] = x_ref[...] + jnp.dot(merged.astype(BF16), wm_ref[...], preferred_element_type=F32)


def _merge(x, c_act, o_attn, gates, w_conv_out, w_diff_out, w_mix_out, tm=256):
    n, d = x.shape
    blocks = (2 * _nbytes((tm, d), F32) + 2 * _nbytes((tm, CONV_CH), BF16) + 2 * _nbytes((tm, d), BF16)
              + 2 * _nbytes((CONV_CH, d), BF16) + _nbytes((d, d), BF16) + 3 * _nbytes((tm, d), F32))
    full = lambda shape: pl.BlockSpec(shape, lambda i: (0, 0))
    return pl.pallas_call(
        _merge_kernel,
        out_shape=jax.ShapeDtypeStruct((n, d), F32),
        grid=(n // tm,),
        in_specs=[pl.BlockSpec((tm, d), lambda i: (i, 0)),
                  pl.BlockSpec((tm, CONV_CH), lambda i: (i, 0)),
                  pl.BlockSpec((tm, QK_W), lambda i: (i, 0)),
                  pl.BlockSpec((tm, d), lambda i: (i, 0)),
                  pl.BlockSpec((tm, d), lambda i: (i, 1)),
                  full((CONV_CH, d)), full((QK_W, d)), full((d, d))],
        out_specs=pl.BlockSpec((tm, d), lambda i: (i, 0)),
        compiler_params=pltpu.CompilerParams(
            dimension_semantics=("parallel",),
            vmem_limit_bytes=_vmem_limit(blocks)),
        name="merge",
    )(x, c_act, o_attn, gates, gates, w_conv_out, w_diff_out, w_mix_out)


def _mem_kv_kernel(mem_ref, g_ref, w_ref, kv_ref):
    mn = _rms(mem_ref[...], g_ref[...]).astype(BF16)
    kv_ref[...] = jnp.dot(mn, w_ref[...], preferred_element_type=F32).astype(kv_ref.dtype)


def _mem_kv(mem, g, w_kv, tm=256):
    n, d = mem.shape
    wkv = w_kv.shape[1]
    blocks = _nbytes((tm, d), F32) + _nbytes((d, wkv), BF16) + _nbytes((tm, wkv), BF16)
    return pl.pallas_call(
        _mem_kv_kernel,
        out_shape=jax.ShapeDtypeStruct((n, wkv), BF16),
        grid=(n // tm,),
        in_specs=[pl.BlockSpec((tm, d), lambda i: (i, 0)),
                  pl.BlockSpec((1, d), lambda i: (0, 0)),
                  pl.BlockSpec((d, wkv), lambda i: (0, 0))],
        out_specs=pl.BlockSpec((tm, wkv), lambda i: (i, 0)),
        compiler_params=pltpu.CompilerParams(
            dimension_semantics=("parallel",),
            vmem_limit_bytes=_vmem_limit(blocks)),
        name="mem_kv",
    )(mem, g.reshape(1, d), w_kv)


def _cross_kernel(x_ref, gx_ref, wq_ref, k_ref, v_ref, wo_ref, gn_ref, xo_ref, xno_ref):
    x = x_ref[...]
    xq = _rms(x, gx_ref[...]).astype(BF16)
    q = jnp.dot(xq, wq_ref[...], preferred_element_type=F32).astype(BF16)
    heads = []
    for h in range(MEM_HEADS):
        lo, hi = h * MEM_HD, (h + 1) * MEM_HD
        s = lax.dot_general(q[:, lo:hi], k_ref[:, lo:hi], (((1,), (1,)), ((), ())),
                            preferred_element_type=F32) * (MEM_HD ** -0.5)
        p = jnp.exp(s - jnp.max(s, axis=-1, keepdims=True))
        p = p / jnp.sum(p, axis=-1, keepdims=True)
        heads.append(jnp.dot(p.astype(BF16), v_ref[:, lo:hi], preferred_element_type=F32))
    o = jnp.concatenate(heads, axis=-1).astype(BF16)
    x_new = x + jnp.dot(o, wo_ref[...], preferred_element_type=F32)
    xo_ref[...] = x_new
    xno_ref[...] = _rms(x_new, gn_ref[...]).astype(xno_ref.dtype)


def _cross_attention(x, kv, g_x, w_q, w_o, g_next, batch, tm=512):
    n, d = x.shape
    mem_len = kv.shape[0] // batch
    nt = n // batch // tm
    blocks = (2 * _nbytes((tm, d), F32) + _nbytes((tm, d), BF16) + 2 * _nbytes((d, MEM_W), BF16)
              + 2 * _nbytes((mem_len, MEM_W), BF16) + 2 * _nbytes((tm, d), F32))
    vec = pl.BlockSpec((1, d), lambda b, i: (0, 0))
    return pl.pallas_call(
        _cross_kernel,
        out_shape=(jax.ShapeDtypeStruct((n, d), F32), jax.ShapeDtypeStruct((n, d), BF16)),
        grid=(batch, nt),
        in_specs=[pl.BlockSpec((tm, d), lambda b, i: (b * nt + i, 0)),
                  vec,
                  pl.BlockSpec((d, MEM_W), lambda b, i: (0, 0)),
                  pl.BlockSpec((mem_len, MEM_W), lambda b, i: (b, 0)),
                  pl.BlockSpec((mem_len, MEM_W), lambda b, i: (b, 1)),
                  pl.BlockSpec((MEM_W, d), lambda b, i: (0, 0)),
                  vec],
        out_specs=(pl.BlockSpec((tm, d), lambda b, i: (b * nt + i, 0)),
                   pl.BlockSpec((tm, d), lambda b, i: (b * nt + i, 0))),
        compiler_params=pltpu.CompilerParams(
            dimension_semantics=("parallel", "parallel"),
            vmem_limit_bytes=_vmem_limit(blocks)),
        name="cross_attention",
    )(x, g_x.reshape(1, d), w_q, kv, kv, w_o, g_next.reshape(1, d))


def kernel(x, mem, positions, ffn1_norm, ffn1_w_in, ffn1_w_out, mix_norm, mix_w_in, conv_w, conv_b, conv_ln_g, conv_ln_b, conv_w_out, diff_lambda, diff_subln_g, diff_w_out, mix_w_out, cross_norm, mem_norm, cross_w_q, cross_w_kv, cross_w_o, ffn2_norm, ffn2_w_in, ffn2_w_out, final_norm):
    batch, seq, d = x.shape
    depth = ffn1_norm.shape[0]
    n = batch * seq
    xr = x.reshape(n, d)
    memr = mem.reshape(batch * mem.shape[1], d)

    inv_freq = ROPE_THETA ** (-jnp.arange(0, HEAD_DIM, 2, dtype=F32) / HEAD_DIM)
    inv_freq2 = jnp.concatenate([inv_freq, inv_freq]).reshape(1, HEAD_DIM)
    cs, sn = _rope_tables(positions.reshape(n, 1), inv_freq2)

    bf = lambda w: w.astype(BF16)
    xn = _rmsnorm(xr, ffn1_norm[0], BF16)
    out = None
    for l in range(depth):
        xr, xn = _ffn(xr, xn, bf(ffn1_w_in[l]), bf(ffn1_w_out[l]), mix_norm[l], BF16)

        w_mix = bf(mix_w_in[l])
        g = _proj_glu(xn, w_mix)
        qk = _proj_rope(xn, w_mix, cs, sn, col0=2 * CONV_CH)
        v = _proj_plain(xn, w_mix, col0=2 * CONV_CH + 2 * QK_W, width=QK_W, gate=False)
        gates = _proj_plain(xn, w_mix, col0=2 * CONV_CH + 3 * QK_W, width=2 * d, gate=True)

        c_act = _conv_branch(g, conv_w[l], conv_b[l], conv_ln_g[l], conv_ln_b[l], batch)
        lam_init = 0.8 - 0.6 * math.exp(-0.3 * l)
        o_attn = _diff_attention(qk, v, diff_lambda[l], diff_subln_g[l], lam_init, batch)
        xr = _merge(xr, c_act, o_attn, gates, bf(conv_w_out[l]), bf(diff_w_out[l]), bf(mix_w_out[l]))

        kv = _mem_kv(memr, mem_norm[l], bf(cross_w_kv[l]))
        xr, xn = _cross_attention(xr, kv, cross_norm[l], bf(cross_w_q[l]), bf(cross_w_o[l]),
                                  ffn2_norm[l], batch)

        last = l == depth - 1
        g_next = final_norm if last else ffn1_norm[l + 1]
        xr, xn = _ffn(xr, xn, bf(ffn2_w_in[l]), bf(ffn2_w_out[l]), g_next, F32 if last else BF16)
        out = xn
    return out.reshape(batch, seq, d)
```

```python
import functools
import math

import jax
import jax.numpy as jnp
from jax import lax
from jax.experimental import pallas as pl
from jax.experimental.pallas import tpu as pltpu

F32 = jnp.float32
BF16 = jnp.bfloat16

CONV_CH = 1024
CONV_K = 31
N_HEADS = 4
HEAD_DIM = 128
HEAD_W = 2 * HEAD_DIM
QK_W = N_HEADS * HEAD_W
MEM_HEADS = 4
MEM_HD = 128
MEM_W = MEM_HEADS * MEM_HD
ROPE_THETA = 10000.0
EPS = 1e-6
LN_EPS = 1e-5
LOG2_E = math.log2(math.e)

LANES_V7X = 128
SUBLANES_V7X = 8
VMEM_BYTES_V7X = 64 * 1024 * 1024
CONV_HALO = 32
ATTN_TQ = 1024
ATTN_TK = 512


def _vmem_limit(block_bytes):
    want = 2 * block_bytes + 12 * 1024 * 1024
    return int(min(want, VMEM_BYTES_V7X - 6 * 1024 * 1024))


def _nbytes(shape, dtype):
    return math.prod(shape) * jnp.dtype(dtype).itemsize


def _rms(xf, g):
    ms = jnp.mean(xf * xf, axis=-1, keepdims=True)
    return xf * lax.rsqrt(ms + EPS) * g


def _sigmoid(x):
    return 1.0 / (1.0 + jnp.exp(-x))


def _rmsnorm_kernel(x_ref, g_ref, o_ref):
    o_ref[...] = _rms(x_ref[...], g_ref[...]).astype(o_ref.dtype)


def _rmsnorm(x, g, out_dtype, tm=512):
    n, d = x.shape
    return pl.pallas_call(
        _rmsnorm_kernel,
        out_shape=jax.ShapeDtypeStruct((n, d), out_dtype),
        grid=(n // tm,),
        in_specs=[pl.BlockSpec((tm, d), lambda i: (i, 0)),
                  pl.BlockSpec((1, d), lambda i: (0, 0))],
        out_specs=pl.BlockSpec((tm, d), lambda i: (i, 0)),
        compiler_params=pltpu.CompilerParams(
            dimension_semantics=("parallel",),
            vmem_limit_bytes=_vmem_limit(_nbytes((tm, d), F32) + _nbytes((tm, d), out_dtype))),
        name="rmsnorm",
    )(x, g.reshape(1, d))


def _rope_table_kernel(pos_ref, inv_ref, cs_ref, sn_ref):
    ang = pos_ref[...].astype(F32) * inv_ref[...]
    lane = lax.broadcasted_iota(jnp.int32, ang.shape, 1)
    s = jnp.sin(ang)
    cs_ref[...] = jnp.cos(ang)
    sn_ref[...] = jnp.where(lane < HEAD_DIM // 2, -s, s)


def _rope_tables(pos, inv_freq2, tm=1024):
    n = pos.shape[0]
    spec = pl.BlockSpec((tm, HEAD_DIM), lambda i: (i, 0))
    return pl.pallas_call(
        _rope_table_kernel,
        out_shape=(jax.ShapeDtypeStruct((n, HEAD_DIM), F32),) * 2,
        grid=(n // tm,),
        in_specs=[pl.BlockSpec((tm, 1), lambda i: (i, 0)),
                  pl.BlockSpec((1, HEAD_DIM), lambda i: (0, 0))],
        out_specs=(spec, spec),
        compiler_params=pltpu.CompilerParams(dimension_semantics=("parallel",)),
        name="rope_tables",
    )(pos, inv_freq2)


def _ffn_kernel(x_ref, xn_ref, wa_ref, wb_ref, wo_ref, g_ref, xo_ref, xno_ref, *, nj):
    j = pl.program_id(1)

    @pl.when(j == 0)
    def _():
        xo_ref[...] = x_ref[...]

    xn = xn_ref[...]
    a = jnp.dot(xn, wa_ref[...], preferred_element_type=F32)
    b = jnp.dot(xn, wb_ref[...], preferred_element_type=F32)
    act = ((0.5 * a) * _sigmoid(a) * b).astype(BF16)
    xo_ref[...] += jnp.dot(act, wo_ref[...], preferred_element_type=F32)

    @pl.when(j == nj - 1)
    def _():
        xno_ref[...] = _rms(xo_ref[...], g_ref[...]).astype(xno_ref.dtype)


def _ffn(x, xn, w_in, w_out, g_next, xn_dtype, tm=512, tf=512):
    n, d = x.shape
    f = w_out.shape[0]
    nj = f // tf
    buffers = 2 * (2 * _nbytes((tm, d), F32) + _nbytes((tm, d), BF16) + _nbytes((tm, d), xn_dtype)
                   + 2 * _nbytes((d, tf), BF16) + _nbytes((tf, d), BF16))
    temps = 2 * _nbytes((tm, tf), F32) + _nbytes((tm, tf), BF16)
    return pl.pallas_call(
        functools.partial(_ffn_kernel, nj=nj),
        out_shape=(jax.ShapeDtypeStruct((n, d), F32), jax.ShapeDtypeStruct((n, d), xn_dtype)),
        grid=(n // tm, nj),
        in_specs=[pl.BlockSpec((tm, d), lambda i, j: (i, 0)),
                  pl.BlockSpec((tm, d), lambda i, j: (i, 0)),
                  pl.BlockSpec((d, tf), lambda i, j: (0, j)),
                  pl.BlockSpec((d, tf), lambda i, j: (0, j + nj)),
                  pl.BlockSpec((tf, d), lambda i, j: (j, 0)),
                  pl.BlockSpec((1, d), lambda i, j: (0, 0))],
        out_specs=(pl.BlockSpec((tm, d), lambda i, j: (i, 0)),
                   pl.BlockSpec((tm, d), lambda i, j: (i, 0))),
        compiler_params=pltpu.CompilerParams(
            dimension_semantics=("parallel", "arbitrary"),
            vmem_limit_bytes=buffers + 4 * temps),
        name="ffn",
    )(x, xn, w_in, w_in, w_out, g_next.reshape(1, d))


def _glu_kernel(xn_ref, wa_ref, wb_ref, o_ref):
    xn = xn_ref[...]
    a = jnp.dot(xn, wa_ref[...], preferred_element_type=F32)
    b = jnp.dot(xn, wb_ref[...], preferred_element_type=F32)
    o_ref[...] = (a * _sigmoid(b)).astype(o_ref.dtype)


def _proj_glu(xn, w, tm=1024, tn=512):
    n, d = xn.shape
    nb = CONV_CH // tn
    blocks = _nbytes((tm, d), BF16) + 2 * _nbytes((d, tn), BF16) + _nbytes((tm, tn), F32)
    return pl.pallas_call(
        _glu_kernel,
        out_shape=jax.ShapeDtypeStruct((n, CONV_CH), F32),
        grid=(n // tm, nb),
        in_specs=[pl.BlockSpec((tm, d), lambda i, j: (i, 0)),
                  pl.BlockSpec((d, tn), lambda i, j: (0, j)),
                  pl.BlockSpec((d, tn), lambda i, j: (0, j + nb))],
        out_specs=pl.BlockSpec((tm, tn), lambda i, j: (i, j)),
        compiler_params=pltpu.CompilerParams(
            dimension_semantics=("parallel", "arbitrary"),
            vmem_limit_bytes=_vmem_limit(blocks)),
        name="proj_glu",
    )(xn, w, w)


def _rope_proj_kernel(xn_ref, w_ref, cs_ref, sn_ref, o_ref, *, n_q_blocks):
    j = pl.program_id(1)
    t = jnp.dot(xn_ref[...], w_ref[...], preferred_element_type=F32)
    cs = cs_ref[...]
    sn = sn_ref[...]
    scale = jnp.where(j < n_q_blocks, LOG2_E * HEAD_DIM ** -0.5, 1.0).astype(F32)
    for c in range(t.shape[1] // HEAD_DIM):
        tc = t[:, c * HEAD_DIM:(c + 1) * HEAD_DIM]
        r = tc * cs + pltpu.roll(tc, HEAD_DIM // 2, 1) * sn
        o_ref[:, c * HEAD_DIM:(c + 1) * HEAD_DIM] = (r * scale).astype(o_ref.dtype)


def _proj_rope(xn, w, cs, sn, col0, tm=1024, tn=1024):
    n, d = xn.shape
    width = 2 * QK_W
    blocks = (_nbytes((tm, d), BF16) + _nbytes((d, tn), BF16) + _nbytes((tm, tn), BF16)
              + 2 * _nbytes((tm, HEAD_DIM), F32) + _nbytes((tm, tn), F32))
    return pl.pallas_call(
        functools.partial(_rope_proj_kernel, n_q_blocks=QK_W // tn),
        out_shape=jax.ShapeDtypeStruct((n, width), BF16),
        grid=(n // tm, width // tn),
        in_specs=[pl.BlockSpec((tm, d), lambda i, j: (i, 0)),
                  pl.BlockSpec((d, tn), lambda i, j: (0, col0 // tn + j)),
                  pl.BlockSpec((tm, HEAD_DIM), lambda i, j: (i, 0)),
                  pl.BlockSpec((tm, HEAD_DIM), lambda i, j: (i, 0))],
        out_specs=pl.BlockSpec((tm, tn), lambda i, j: (i, j)),
        compiler_params=pltpu.CompilerParams(
            dimension_semantics=("parallel", "arbitrary"),
            vmem_limit_bytes=_vmem_limit(blocks)),
        name="proj_rope",
    )(xn, w, cs, sn)


def _gate_proj_kernel(xn_ref, w_ref, o_ref):
    t = jnp.dot(xn_ref[...], w_ref[...], preferred_element_type=F32)
    o_ref[...] = _sigmoid(t).astype(o_ref.dtype)


def _proj_gates(xn, w, col0, width, tm=1024, tn=1024):
    n, d = xn.shape
    blocks = (_nbytes((tm, d), BF16) + _nbytes((d, tn), BF16) + _nbytes((tm, tn), BF16)
              + _nbytes((tm, tn), F32))
    return pl.pallas_call(
        _gate_proj_kernel,
        out_shape=jax.ShapeDtypeStruct((n, width), BF16),
        grid=(n // tm, width // tn),
        in_specs=[pl.BlockSpec((tm, d), lambda i, j: (i, 0)),
                  pl.BlockSpec((d, tn), lambda i, j: (0, col0 // tn + j))],
        out_specs=pl.BlockSpec((tm, tn), lambda i, j: (i, j)),
        compiler_params=pltpu.CompilerParams(
            dimension_semantics=("parallel", "arbitrary"),
            vmem_limit_bytes=_vmem_limit(blocks)),
        name="proj_gate",
    )(xn, w)


def _vt_proj_kernel(xn_ref, wt_ref, o_ref):
    o_ref[0] = lax.dot_general(wt_ref[...], xn_ref[...], (((1,), (1,)), ((), ())),
                               preferred_element_type=F32).astype(o_ref.dtype)


def _proj_vt(xn, w_t, tk):
    n, d = xn.shape
    width = w_t.shape[0]
    blocks = (_nbytes((tk, d), BF16) + _nbytes((width, d), BF16) + _nbytes((width, tk), BF16)
              + _nbytes((width, tk), F32))
    return pl.pallas_call(
        _vt_proj_kernel,
        out_shape=jax.ShapeDtypeStruct((n // tk, width, tk), BF16),
        grid=(n // tk,),
        in_specs=[pl.BlockSpec((tk, d), lambda i: (i, 0)),
                  pl.BlockSpec((width, d), lambda i: (0, 0))],
        out_specs=pl.BlockSpec((1, width, tk), lambda i: (i, 0, 0)),
        compiler_params=pltpu.CompilerParams(
            dimension_semantics=("parallel",),
            vmem_limit_bytes=_vmem_limit(blocks)),
        name="proj_vt",
    )(xn, w_t)


def _conv_kernel(halo_ref, g_ref, w_ref, b_ref, lng_ref, lnb_ref, o_ref, cat_ref, y_ref, *, tm, rc, cc):
    i = pl.program_id(1)
    cat_ref[0:CONV_HALO, :] = jnp.where(i == 0, 0.0, halo_ref[...])
    cat_ref[CONV_HALO:CONV_HALO + tm, :] = g_ref[...]
    first = CONV_HALO - (CONV_K - 1)
    win = rc + CONV_HALO
    for r0 in range(0, tm, rc):
        for c0 in range(0, CONV_CH, cc):
            window = cat_ref[r0:r0 + win, c0:c0 + cc]
            acc = jnp.zeros((rc, cc), F32)
            for phase in range(SUBLANES_V7X):
                taps = [k for k in range(CONV_K) if (first + k) % SUBLANES_V7X == phase]
                shifted = window if phase == 0 else pltpu.roll(window, win - phase, 0)
                for k in taps:
                    base = first + k - phase
                    acc = acc + w_ref[k:k + 1, c0:c0 + cc] * shifted[base:base + rc]
            y_ref[r0:r0 + rc, c0:c0 + cc] = acc + b_ref[:, c0:c0 + cc]
    y = y_ref[...]
    mu = jnp.mean(y, axis=-1, keepdims=True)
    yc = y - mu
    var = jnp.mean(yc * yc, axis=-1, keepdims=True)
    yn = yc * lax.rsqrt(var + LN_EPS) * lng_ref[...] + lnb_ref[...]
    o_ref[...] = (yn * _sigmoid(yn)).astype(o_ref.dtype)


def _conv_branch(g, conv_w, conv_b, ln_g, ln_b, batch, tm=256, rc=64, cc=256):
    n, c = g.shape
    seq = n // batch
    nt = seq // tm
    hb = tm // CONV_HALO
    blocks = (_nbytes((tm, c), F32) + _nbytes((CONV_HALO, c), F32) + _nbytes((tm, c), BF16)
              + _nbytes((CONV_HALO + tm, c), F32) + _nbytes((tm, c), F32))
    row = lambda v: v.reshape(1, c)
    vec = pl.BlockSpec((1, c), lambda b, i: (0, 0))
    return pl.pallas_call(
        functools.partial(_conv_kernel, tm=tm, rc=rc, cc=cc),
        out_shape=jax.ShapeDtypeStruct((n, c), BF16),
        grid=(batch, nt),
        in_specs=[pl.BlockSpec((CONV_HALO, c), lambda b, i: (jnp.maximum((b * nt + i) * hb - 1, 0), 0)),
                  pl.BlockSpec((tm, c), lambda b, i: (b * nt + i, 0)),
                  pl.BlockSpec((CONV_K, c), lambda b, i: (0, 0)),
                  vec, vec, vec],
        out_specs=pl.BlockSpec((tm, c), lambda b, i: (b * nt + i, 0)),
        scratch_shapes=[pltpu.VMEM((CONV_HALO + tm, c), F32), pltpu.VMEM((tm, c), F32)],
        compiler_params=pltpu.CompilerParams(
            dimension_semantics=("parallel", "parallel"),
            vmem_limit_bytes=_vmem_limit(blocks)),
        name="conv_branch",
    )(g, g, conv_w, row(conv_b), row(ln_g), row(ln_b))


def _attn_kernel(lam_ref, gsub_ref, q_ref, k_ref, vt_ref, o_ref, m_ref, l_ref, acc_ref, s_ref, *, tq, tk, qc, lam_init):
    i = pl.program_id(2)
    ratio = tq // tk
    m_ref[...] = jnp.full(m_ref.shape, -jnp.inf, F32)
    l_ref[...] = jnp.zeros(l_ref.shape, F32)
    acc_ref[...] = jnp.zeros(acc_ref.shape, F32)

    def chains(diag):
        return [(mp, c0) for mp in range(2) for c0 in range(0, tq, qc)
                if diag is None or c0 + qc > diag * tk]

    def scores(j, u, diag):
        start = pl.multiple_of(j * tk, tk)
        for mp, c0 in chains(diag):
            lo, hi = mp * HEAD_DIM, (mp + 1) * HEAD_DIM
            s_ref[u % 2, mp, :, c0:c0 + qc] = lax.dot_general(
                k_ref[pl.ds(start, tk), lo:hi], q_ref[c0:c0 + qc, lo:hi],
                (((1,), (1,)), ((), ())), preferred_element_type=F32)

    def update(j, u, diag):
        vt = vt_ref[j]
        for mp, c0 in chains(diag):
            st = s_ref[u % 2, mp, :, c0:c0 + qc]
            if diag is not None and c0 < (diag + 1) * tk:
                kv_pos = lax.broadcasted_iota(jnp.int32, st.shape, 0) + diag * tk
                q_pos = lax.broadcasted_iota(jnp.int32, st.shape, 1) + c0
                st = jnp.where(kv_pos <= q_pos, st, -jnp.inf)
            m_old = m_ref[mp, :, c0:c0 + qc]
            m_new = jnp.maximum(m_old, jnp.max(st, axis=0, keepdims=True))
            p = jnp.exp2(st - m_new)
            alpha = jnp.exp2(m_old - m_new)
            l_ref[mp, :, c0:c0 + qc] = alpha * l_ref[mp, :, c0:c0 + qc] + jnp.sum(p, axis=0, keepdims=True)
            acc_ref[mp, :, c0:c0 + qc] = (alpha * acc_ref[mp, :, c0:c0 + qc]
                                          + jnp.dot(vt, p.astype(BF16), preferred_element_type=F32))
            m_ref[mp, :, c0:c0 + qc] = m_new

    scores(0, 0, None)

    def body(t, carry):
        for u in range(ratio):
            j = ratio * t + u
            scores(j + 1, (u + 1) % ratio, None)
            update(j, u, None)
        return carry

    lax.fori_loop(0, i, body, 0)
    for d in range(ratio):
        j = ratio * i + d
        if d + 1 < ratio:
            scores(j + 1, d + 1, d + 1)
        update(j, d, d)

    lp = lam_ref[...]
    lam = (jnp.exp(jnp.sum(lp[0:1] * lp[1:2], axis=-1, keepdims=True))
           - jnp.exp(jnp.sum(lp[2:3] * lp[3:4], axis=-1, keepdims=True)) + lam_init)
    ot = acc_ref[0] * (1.0 / l_ref[0]) - lam * (acc_ref[1] * (1.0 / l_ref[1]))
    ms = jnp.mean(ot * ot, axis=0, keepdims=True)
    on = ot * lax.rsqrt(ms + EPS) * gsub_ref[...] * (1.0 - lam_init)
    o_ref[...] = on.T.astype(o_ref.dtype)


def _diff_attention(qk, vt, lam_p, g_sub, lam_init, batch, tq, qc=256):
    n = qk.shape[0]
    tk = vt.shape[2]
    assert tq % (2 * tk) == 0 and tk % qc == 0
    seq = n // batch
    nq = seq // tq
    blocks = (2 * _nbytes((tq, HEAD_W), BF16) + 2 * _nbytes((seq, HEAD_W), BF16)
              + 3 * _nbytes((2, HEAD_W, tq), F32) + 4 * _nbytes((tk, tq), F32))
    return pl.pallas_call(
        functools.partial(_attn_kernel, tq=tq, tk=tk, qc=qc, lam_init=lam_init),
        out_shape=jax.ShapeDtypeStruct((n, QK_W), BF16),
        grid=(batch, N_HEADS, nq),
        in_specs=[pl.BlockSpec((4, HEAD_DIM), lambda b, h, i: (0, 0)),
                  pl.BlockSpec((HEAD_W, 1), lambda b, h, i: (0, 0)),
                  pl.BlockSpec((tq, HEAD_W), lambda b, h, i: (b * nq + i, h)),
                  pl.BlockSpec((seq, HEAD_W), lambda b, h, i: (b, N_HEADS + h)),
                  pl.BlockSpec((seq // tk, HEAD_W, tk), lambda b, h, i: (b, h, 0))],
        out_specs=pl.BlockSpec((tq, HEAD_W), lambda b, h, i: (b * nq + i, h)),
        scratch_shapes=[pltpu.VMEM((2, 1, tq), F32), pltpu.VMEM((2, 1, tq), F32),
                        pltpu.VMEM((2, HEAD_W, tq), F32), pltpu.VMEM((2, 2, tk, tq), F32)],
        compiler_params=pltpu.CompilerParams(
            dimension_semantics=("parallel", "parallel", "arbitrary"),
            vmem_limit_bytes=_vmem_limit(blocks)),
        name="diff_attention",
    )(lam_p, g_sub.reshape(HEAD_W, 1), qk, qk, vt)


def _merge_kernel(x_ref, c_ref, o_ref, gc_ref, ga_ref, wc_ref, wd_ref, wm_ref, xo_ref):
    y_conv = jnp.dot(c_ref[...], wc_ref[...], preferred_element_type=F32)
    y_attn = jnp.dot(o_ref[...], wd_ref[...], preferred_element_type=F32)
    merged = gc_ref[...].astype(F32) * y_conv + ga_ref[...].astype(F32) * y_attn
    xo_ref[...] = x_ref[...] + jnp.dot(merged.astype(BF16), wm_ref[...], preferred_element_type=F32)


def _merge(x, c_act, o_attn, gates, w_conv_out, w_diff_out, w_mix_out, tm=256):
    n, d = x.shape
    blocks = (2 * _nbytes((tm, d), F32) + 2 * _nbytes((tm, CONV_CH), BF16) + 2 * _nbytes((tm, d), BF16)
              + 2 * _nbytes((CONV_CH, d), BF16) + _nbytes((d, d), BF16) + 3 * _nbytes((tm, d), F32))
    full = lambda shape: pl.BlockSpec(shape, lambda i: (0, 0))
    return pl.pallas_call(
        _merge_kernel,
        out_shape=jax.ShapeDtypeStruct((n, d), F32),
        grid=(n // tm,),
        in_specs=[pl.BlockSpec((tm, d), lambda i: (i, 0)),
                  pl.BlockSpec((tm, CONV_CH), lambda i: (i, 0)),
                  pl.BlockSpec((tm, QK_W), lambda i: (i, 0)),
                  pl.BlockSpec((tm, d), lambda i: (i, 0)),
                  pl.BlockSpec((tm, d), lambda i: (i, 1)),
                  full((CONV_CH, d)), full((QK_W, d)), full((d, d))],
        out_specs=pl.BlockSpec((tm, d), lambda i: (i, 0)),
        compiler_params=pltpu.CompilerParams(
            dimension_semantics=("parallel",),
            vmem_limit_bytes=_vmem_limit(blocks)),
        name="merge",
    )(x, c_act, o_attn, gates, gates, w_conv_out, w_diff_out, w_mix_out)


def _mem_kv_kernel(mem_ref, g_ref, w_ref, kv_ref):
    mn = _rms(mem_ref[...], g_ref[...]).astype(BF16)
    kv_ref[...] = jnp.dot(mn, w_ref[...], preferred_element_type=F32).astype(kv_ref.dtype)


def _mem_kv(mem, g, w_kv, tm=256):
    n, d = mem.shape
    wkv = w_kv.shape[1]
    blocks = _nbytes((tm, d), F32) + _nbytes((d, wkv), BF16) + _nbytes((tm, wkv), BF16)
    return pl.pallas_call(
        _mem_kv_kernel,
        out_shape=jax.ShapeDtypeStruct((n, wkv), BF16),
        grid=(n // tm,),
        in_specs=[pl.BlockSpec((tm, d), lambda i: (i, 0)),
                  pl.BlockSpec((1, d), lambda i: (0, 0)),
                  pl.BlockSpec((d, wkv), lambda i: (0, 0))],
        out_specs=pl.BlockSpec((tm, wkv), lambda i: (i, 0)),
        compiler_params=pltpu.CompilerParams(
            dimension_semantics=("parallel",),
            vmem_limit_bytes=_vmem_limit(blocks)),
        name="mem_kv",
    )(mem, g.reshape(1, d), w_kv)


def _cross_kernel(x_ref, gx_ref, wq_ref, k_ref, v_ref, wo_ref, gn_ref, xo_ref, xno_ref):
    x = x_ref[...]
    xq = _rms(x, gx_ref[...]).astype(BF16)
    q = jnp.dot(xq, wq_ref[...], preferred_element_type=F32).astype(BF16)
    heads = []
    for h in range(MEM_HEADS):
        lo, hi = h * MEM_HD, (h + 1) * MEM_HD
        s = lax.dot_general(q[:, lo:hi], k_ref[:, lo:hi], (((1,), (1,)), ((), ())),
                            preferred_element_type=F32) * (MEM_HD ** -0.5)
        p = jnp.exp(s - jnp.max(s, axis=-1, keepdims=True))
        p = p / jnp.sum(p, axis=-1, keepdims=True)
        heads.append(jnp.dot(p.astype(BF16), v_ref[:, lo:hi], preferred_element_type=F32))
    o = jnp.concatenate(heads, axis=-1).astype(BF16)
    x_new = x + jnp.dot(o, wo_ref[...], preferred_element_type=F32)
    xo_ref[...] = x_new
    xno_ref[...] = _rms(x_new, gn_ref[...]).astype(xno_ref.dtype)


def _cross_attention(x, kv, g_x, w_q, w_o, g_next, batch, tm=512):
    n, d = x.shape
    mem_len = kv.shape[0] // batch
    nt = n // batch // tm
    blocks = (2 * _nbytes((tm, d), F32) + _nbytes((tm, d), BF16) + 2 * _nbytes((d, MEM_W), BF16)
              + 2 * _nbytes((mem_len, MEM_W), BF16) + 2 * _nbytes((tm, d), F32))
    vec = pl.BlockSpec((1, d), lambda b, i: (0, 0))
    return pl.pallas_call(
        _cross_kernel,
        out_shape=(jax.ShapeDtypeStruct((n, d), F32), jax.ShapeDtypeStruct((n, d), BF16)),
        grid=(batch, nt),
        in_specs=[pl.BlockSpec((tm, d), lambda b, i: (b * nt + i, 0)),
                  vec,
                  pl.BlockSpec((d, MEM_W), lambda b, i: (0, 0)),
                  pl.BlockSpec((mem_len, MEM_W), lambda b, i: (b, 0)),
                  pl.BlockSpec((mem_len, MEM_W), lambda b, i: (b, 1)),
                  pl.BlockSpec((MEM_W, d), lambda b, i: (0, 0)),
                  vec],
        out_specs=(pl.BlockSpec((tm, d), lambda b, i: (b * nt + i, 0)),
                   pl.BlockSpec((tm, d), lambda b, i: (b * nt + i, 0))),
        compiler_params=pltpu.CompilerParams(
            dimension_semantics=("parallel", "parallel"),
            vmem_limit_bytes=_vmem_limit(blocks)),
        name="cross_attention",
    )(x, g_x.reshape(1, d), w_q, kv, kv, w_o, g_next.reshape(1, d))


def kernel(x, mem, positions, ffn1_norm, ffn1_w_in, ffn1_w_out, mix_norm, mix_w_in, conv_w, conv_b, conv_ln_g, conv_ln_b, conv_w_out, diff_lambda, diff_subln_g, diff_w_out, mix_w_out, cross_norm, mem_norm, cross_w_q, cross_w_kv, cross_w_o, ffn2_norm, ffn2_w_in, ffn2_w_out, final_norm):
    batch, seq, d = x.shape
    depth = ffn1_norm.shape[0]
    n = batch * seq
    xr = x.reshape(n, d)
    memr = mem.reshape(batch * mem.shape[1], d)

    inv_freq = ROPE_THETA ** (-jnp.arange(0, HEAD_DIM, 2, dtype=F32) / HEAD_DIM)
    inv_freq2 = jnp.concatenate([inv_freq, inv_freq]).reshape(1, HEAD_DIM)
    cs, sn = _rope_tables(positions.reshape(n, 1), inv_freq2)

    bf = lambda w: w.astype(BF16)
    xn = _rmsnorm(xr, ffn1_norm[0], BF16)
    out = None
    for l in range(depth):
        xr, xn = _ffn(xr, xn, bf(ffn1_w_in[l]), bf(ffn1_w_out[l]), mix_norm[l], BF16)

        w_mix = bf(mix_w_in[l])
        g = _proj_glu(xn, w_mix)
        qk = _proj_rope(xn, w_mix, cs, sn, col0=2 * CONV_CH)
        v_col0 = 2 * CONV_CH + 2 * QK_W
        vt = _proj_vt(xn, bf(mix_w_in[l][:, v_col0:v_col0 + QK_W].T), ATTN_TK)
        gates = _proj_gates(xn, w_mix, col0=v_col0 + QK_W, width=2 * d)

        c_act = _conv_branch(g, conv_w[l], conv_b[l], conv_ln_g[l], conv_ln_b[l], batch)
        lam_init = 0.8 - 0.6 * math.exp(-0.3 * l)
        o_attn = _diff_attention(qk, vt, diff_lambda[l], diff_subln_g[l], lam_init, batch, ATTN_TQ)
        xr = _merge(xr, c_act, o_attn, gates, bf(conv_w_out[l]), bf(diff_w_out[l]), bf(mix_w_out[l]))

        kv = _mem_kv(memr, mem_norm[l], bf(cross_w_kv[l]))
        xr, xn = _cross_attention(xr, kv, cross_norm[l], bf(cross_w_q[l]), bf(cross_w_o[l]),
                                  ffn2_norm[l], batch)

        last = l == depth - 1
        g_next = final_norm if last else ffn1_norm[l + 1]
        xr, xn = _ffn(xr, xn, bf(ffn2_w_in[l]), bf(ffn2_w_out[l]), g_next, F32 if last else BF16)
        out = xn
    return out.reshape(batch, seq, d)
```
